```python
import jax, jax.numpy as jnp
from jax import lax
import numpy as np

D_MODEL = 1024
BATCH = 4
SEQ = 4096
DEPTH = 1
DEC_BATCH = 16
DEC_SEQ = 32
PAST_LEN = 4096

CHUNK = 64
MIX_WIDTH = D_MODEL
ATTN_WIDTH = MIX_WIDTH // 2
SGU_WIDTH = MIX_WIDTH - ATTN_WIDTH
HEAD_DIM = 64
N_HEADS = ATTN_WIDTH // HEAD_DIM
N_IDX_HEADS = 8
IDX_DIM = 64
MAX_TOPK = 256
SGU_GROUPS = 4
SGU_GROUP_DIM = SGU_WIDTH // SGU_GROUPS
SGU_CHUNK = 128
D_FF = -(-8 * D_MODEL // (3 * 256)) * 256
ROPE_THETA = 10000.0
QUERY_BLOCK = 128
RMS_EPS = 1e-6
LN_EPS = 1e-5
PROJ_SIZES = (ATTN_WIDTH, ATTN_WIDTH, ATTN_WIDTH, N_IDX_HEADS * IDX_DIM, IDX_DIM, N_IDX_HEADS, SGU_WIDTH, SGU_WIDTH)
PROJ_WIDTH = sum(PROJ_SIZES)

kernel_name = "hybrid_dsa_sgu_stream_step"


def _rms_norm(x, g):
    xf = x.astype(jnp.float32)
    y = xf * lax.rsqrt(jnp.mean(xf * xf, axis=-1, keepdims=True) + RMS_EPS)
    return (y * g.astype(jnp.float32)).astype(x.dtype)


def _rope(x, pos):
    half = x.shape[-1] // 2
    freqs = ROPE_THETA ** (-jnp.arange(half, dtype=jnp.float32) / half)
    ang = pos.astype(jnp.float32)[:, None] * freqs[None, :]
    cos = jnp.cos(ang)[:, None, :]
    sin = jnp.sin(ang)[:, None, :]
    xf = x.astype(jnp.float32)
    x1, x2 = xf[..., :half], xf[..., half:]
    return jnp.concatenate([x1 * cos - x2 * sin, x2 * cos + x1 * sin], axis=-1).astype(x.dtype)


def _mixer_inputs(xn, w_in, pos):
    B, T, _ = xn.shape
    offsets = np.cumsum(PROJ_SIZES[:-1]).tolist()
    q, k, v, qi, ki, wi, u, vs = jnp.split(xn @ w_in, offsets, axis=-1)
    q = _rope(q.reshape(B, T, N_HEADS, HEAD_DIM), pos)
    k = _rope(k.reshape(B, T, N_HEADS, HEAD_DIM), pos)
    v = v.reshape(B, T, N_HEADS, HEAD_DIM)
    qi = _rope(qi.reshape(B, T, N_IDX_HEADS, IDX_DIM), pos)
    ki = _rope(ki[:, :, None, :], pos)[:, :, 0, :]
    return q, k, v, qi, ki, wi, u, vs


def _sparse_attention(q, qi, wi, pos_q, k, v, ki, pos_k, topk):
    B = q.shape[0]
    idx_scale = (IDX_DIM ** -0.5) * (N_IDX_HEADS ** -0.5)
    rel = jax.nn.relu(jnp.einsum('bthd,bsd->bths', qi, ki).astype(jnp.float32))
    score = jnp.einsum('bths,bth->bts', rel, wi.astype(jnp.float32)) * idx_scale
    allowed = (pos_k[None, :] // CHUNK) <= (pos_q[:, None] // CHUNK)
    score = jnp.where(allowed[None], score, -jnp.inf)
    top_val, top_idx = lax.top_k(score, topk)
    valid = jnp.isfinite(top_val)
    bidx = jnp.arange(B)[:, None, None]
    kg = k[bidx, top_idx]
    vg = v[bidx, top_idx]
    logits = jnp.einsum('bthd,btkhd->bthk', q, kg).astype(jnp.float32) * (HEAD_DIM ** -0.5)
    logits = jnp.where(valid[:, :, None, :], logits, -jnp.inf)
    p = jax.nn.softmax(logits, axis=-1)
    return jnp.einsum('bthk,btkhd->bthd', p.astype(v.dtype), vg)


def _sgu(u, vs, ln_g, ln_b, w_s, b_s):
    B, T, _ = u.shape
    tc = min(T, SGU_CHUNK)
    nc = T // tc
    u = jax.nn.gelu(u).reshape(B, nc, tc, SGU_GROUPS, SGU_GROUP_DIM)
    vs = jax.nn.gelu(vs).reshape(B, nc, tc, SGU_GROUPS, SGU_GROUP_DIM)
    vf = vs.astype(jnp.float32)
    mu = jnp.mean(vf, axis=-1, keepdims=True)
    var = jnp.mean(jnp.square(vf - mu), axis=-1, keepdims=True)
    vn = ((vf - mu) * lax.rsqrt(var + LN_EPS) * ln_g.astype(jnp.float32) + ln_b.astype(jnp.float32)).astype(u.dtype)
    ii = jnp.arange(tc)
    mask = (ii[None, :] // CHUNK) <= (ii[:, None] // CHUNK)
    wm = jnp.where(mask[None], w_s[:, :tc, :tc], 0.0)
    s = jnp.einsum('gij,bcjge->bcige', wm, vn) + b_s[:, :tc].T[:, :, None]
    out = (u * s).reshape(B, T, SGU_WIDTH)
    return out, vn.reshape(B, T, SGU_GROUPS, SGU_GROUP_DIM)


def _layer(x, pos, past, g_attn, w_in, ln_g, ln_b, w_s, b_s, w_out, g_ffn, w_gate, w_up, w_down):
    B, T, _ = x.shape
    xn = _rms_norm(x, g_attn)
    q, k, v, qi, ki, wi, u, vs = _mixer_inputs(xn, w_in, pos)
    if past is None:
        topk = min(MAX_TOPK, T // 4)
        nb = T // QUERY_BLOCK

        def to_blocks(a):
            return a.reshape((B, nb, QUERY_BLOCK) + a.shape[2:]).swapaxes(0, 1)

        def blk(args):
            qb, qib, wib, pb = args
            return _sparse_attention(qb, qib, wib, pb, k, v, ki, pos, topk)

        attn = lax.map(blk, (to_blocks(q), to_blocks(qi), to_blocks(wi), pos.reshape(nb, QUERY_BLOCK)))
        attn = attn.swapaxes(0, 1).reshape(B, T, ATTN_WIDTH)
    else:
        k_past, v_past, ki_past = past
        S = k_past.shape[1] + T
        topk = min(MAX_TOPK, S // 4)
        k_all = jnp.concatenate([k_past, k], axis=1)
        v_all = jnp.concatenate([v_past, v], axis=1)
        ki_all = jnp.concatenate([ki_past, ki], axis=1)
        pos_k = jnp.arange(S, dtype=jnp.int32)
        attn = _sparse_attention(q, qi, wi, pos, k_all, v_all, ki_all, pos_k, topk).reshape(B, T, ATTN_WIDTH)
    sgu, vn = _sgu(u, vs, ln_g, ln_b, w_s, b_s)
    h = x + jnp.concatenate([attn, sgu], axis=-1) @ w_out
    hn = _rms_norm(h, g_ffn)
    y = h + (jax.nn.silu(hn @ w_gate) * (hn @ w_up)) @ w_down
    return y, k, v, ki, vn


def setup_inputs(seed: int = 0) -> dict:
    key = jax.random.key(seed)
    ks = jax.random.split(key, 20)
    f32 = jnp.float32
    nrm = lambda k, shape, s: jax.random.normal(k, shape, f32) * s
    return {
        "x_prompt": nrm(ks[0], (BATCH, SEQ, D_MODEL), 1.0),
        "x_sample": nrm(ks[1], (DEC_BATCH, DEC_SEQ, D_MODEL), 1.0),
        "cache_k": nrm(ks[2], (DEPTH, DEC_BATCH, PAST_LEN, N_HEADS, HEAD_DIM), 1.0),
        "cache_v": nrm(ks[3], (DEPTH, DEC_BATCH, PAST_LEN, N_HEADS, HEAD_DIM), 1.0),
        "cache_idx_k": nrm(ks[4], (DEPTH, DEC_BATCH, PAST_LEN, IDX_DIM), 1.0),
        "g_attn": 1.0 + nrm(ks[5], (DEPTH, D_MODEL), 0.1),
        "w_in": nrm(ks[6], (DEPTH, D_MODEL, PROJ_WIDTH), D_MODEL ** -0.5),
        "ln_g": 1.0 + nrm(ks[7], (DEPTH, SGU_GROUPS, SGU_GROUP_DIM), 0.1),
        "ln_b": nrm(ks[8], (DEPTH, SGU_GROUPS, SGU_GROUP_DIM), 0.1),
        "w_s": nrm(ks[9], (DEPTH, SGU_GROUPS, SGU_CHUNK, SGU_CHUNK), SGU_CHUNK ** -0.5),
        "b_s": 1.0 + nrm(ks[10], (DEPTH, SGU_GROUPS, SGU_CHUNK), 0.1),
        "w_out": nrm(ks[11], (DEPTH, MIX_WIDTH, D_MODEL), MIX_WIDTH ** -0.5),
        "g_ffn": 1.0 + nrm(ks[12], (DEPTH, D_MODEL), 0.1),
        "w_gate": nrm(ks[13], (DEPTH, D_MODEL, D_FF), D_MODEL ** -0.5),
        "w_up": nrm(ks[14], (DEPTH, D_MODEL, D_FF), D_MODEL ** -0.5),
        "w_down": nrm(ks[15], (DEPTH, D_FF, D_MODEL), D_FF ** -0.5),
        "g_final": 1.0 + nrm(ks[16], (D_MODEL,), 0.1),
    }


def reference(x_prompt, x_sample, cache_k, cache_v, cache_idx_k, g_attn, w_in, ln_g, ln_b, w_s, b_s,
              w_out, g_ffn, w_gate, w_up, w_down, g_final):
    T_p = x_prompt.shape[1]
    P = cache_k.shape[2]
    T_s = x_sample.shape[1]
    pos_p = jnp.arange(T_p, dtype=jnp.int32)
    pos_s = P + jnp.arange(T_s, dtype=jnp.int32)
    h_p, h_s = x_prompt, x_sample
    kp, vp, kip, ks_, vs_, kis, sgs = [], [], [], [], [], [], []
    for l in range(DEPTH):
        wts = (g_attn[l], w_in[l], ln_g[l], ln_b[l], w_s[l], b_s[l], w_out[l], g_ffn[l], w_gate[l], w_up[l], w_down[l])
        h_p, k1, v1, ki1, _vn_p = _layer(h_p, pos_p, None, *wts)
        h_s, k2, v2, ki2, vn_s = _layer(h_s, pos_s, (cache_k[l], cache_v[l], cache_idx_k[l]), *wts)
        kp.append(k1); vp.append(v1); kip.append(ki1)
        ks_.append(k2); vs_.append(v2); kis.append(ki2); sgs.append(vn_s)
    y_prompt = _rms_norm(h_p, g_final)
    y_sample = _rms_norm(h_s, g_final)
    return (y_prompt, y_sample, jnp.stack(kp), jnp.stack(vp), jnp.stack(kip),
            jnp.stack(ks_), jnp.stack(vs_), jnp.stack(kis), jnp.stack(sgs))
```

```python
import functools

import jax
import jax.numpy as jnp
import numpy as np
from jax import lax
from jax.experimental import pallas as pl
from jax.experimental.pallas import tpu as pltpu

D_MODEL = 1024
CHUNK = 64
HEAD_DIM = 64
N_HEADS = 8
ATTN_WIDTH = N_HEADS * HEAD_DIM
N_IDX_HEADS = 8
IDX_DIM = 64
MAX_TOPK = 256
SGU_GROUPS = 4
SGU_GROUP_DIM = 128
SGU_WIDTH = SGU_GROUPS * SGU_GROUP_DIM
SGU_CHUNK = 128
D_FF = 2816
ROPE_THETA = 10000.0
RMS_EPS = 1e-6
LN_EPS = 1e-5

LANES = 128
VMEM_LIMIT_BYTES = 56 * 1024 * 1024

TOKEN_TILE = 512
KEY_TILE = 512
MAX_ROW_GROUP = 64
FF_TILE = 256

NEG_INF_KEY = -2139095041
MASK_BIAS = -1e30
IDX_SCALE = (IDX_DIM ** -0.5) * (N_IDX_HEADS ** -0.5)
BF16 = jnp.bfloat16
F32 = jnp.float32


def _dot(a, b):
    return jnp.dot(a, b, preferred_element_type=F32)


def _dot_nt(a, b):
    return lax.dot_general(a, b, (((1,), (1,)), ((), ())), preferred_element_type=F32)


def _rms(x, g):
    return x * lax.rsqrt(jnp.mean(x * x, axis=-1, keepdims=True) + RMS_EPS) * g


def _gelu_tanh(x):
    c = np.float32(np.sqrt(2.0 / np.pi))
    return 0.5 * x * (1.0 + jnp.tanh(c * (x + np.float32(0.044715) * (x * x * x))))


def _rope_lanes(x, cos, sin_signed, first_half):
    partner = jnp.where(first_half, pltpu.roll(x, 96, 1), pltpu.roll(x, 32, 1))
    return x * cos + partner * sin_signed


def _proj_kernel(x_ref, g_ref, w_ref, ws_ref, cos_ref, sin_ref, lng_ref, lnb_ref, wsgu_ref, bsgu_ref,
                 q_ref, k_ref, v_ref, ki_ref, kb_ref, vb_ref, kib_ref, qi_ref, wi_ref, sgu_ref, vn_ref,
                 *, sgu_chunk):
    tm = x_ref.shape[0]
    xn = _rms(x_ref[...], g_ref[...]).astype(BF16)
    cos = cos_ref[...]
    sin = sin_ref[...]
    lane = lax.broadcasted_iota(jnp.int32, (tm, LANES), 1)
    first_half = (lane % HEAD_DIM) < (HEAD_DIM // 2)

    def rope(r):
        return jnp.concatenate(
            [_rope_lanes(r[:, c * LANES:(c + 1) * LANES], cos, sin, first_half)
             for c in range(r.shape[1] // LANES)], axis=1)

    def section(i):
        return _dot(xn, w_ref[:, i * ATTN_WIDTH:(i + 1) * ATTN_WIDTH])

    q = rope(section(0))
    q_ref[...] = (q * np.float32(HEAD_DIM ** -0.5)).astype(BF16)
    k = rope(section(1))
    k_ref[...] = k
    kb_ref[...] = k.astype(BF16)
    v = section(2)
    v_ref[...] = v
    vb_ref[...] = v.astype(BF16)
    qi_ref[...] = rope(section(3)).astype(BF16)

    small = _dot(xn, ws_ref[...])
    ki = _rope_lanes(small, cos, sin, first_half)[:, :IDX_DIM]
    ki_ref[...] = ki
    kib_ref[...] = ki.astype(BF16)
    wi_ref[...] = small

    u = _gelu_tanh(section(4))
    vs = _gelu_tanh(section(5))
    ii = lax.broadcasted_iota(jnp.int32, (sgu_chunk, sgu_chunk), 0)
    jj = lax.broadcasted_iota(jnp.int32, (sgu_chunk, sgu_chunk), 1)
    causal = (jj // CHUNK) <= (ii // CHUNK)
    for g in range(SGU_GROUPS):
        lanes = slice(g * SGU_GROUP_DIM, (g + 1) * SGU_GROUP_DIM)
        vg = vs[:, lanes]
        mu = jnp.mean(vg, axis=-1, keepdims=True)
        var = jnp.mean(jnp.square(vg - mu), axis=-1, keepdims=True)
        vn = (vg - mu) * lax.rsqrt(var + LN_EPS) * lng_ref[:, lanes] + lnb_ref[:, lanes]
        vn_ref[:, lanes] = vn
        vnb = vn.astype(BF16)
        wm = jnp.where(causal, wsgu_ref[g], 0.0).astype(BF16)
        bias = bsgu_ref[:, g:g + 1]
        for c in range(tm // sgu_chunk):
            rows = slice(c * sgu_chunk, (c + 1) * sgu_chunk)
            s = _dot(wm, vnb[rows]) + bias
            sgu_ref[rows, lanes] = (u[rows, lanes] * s).astype(BF16)


def _project(x2d, g_attn, w_main, w_small, cos_t, sin_t, ln_g, ln_b, w_sgu, b_sgu, *, sgu_chunk, table_tiles):
    n = x2d.shape[0]
    tm = TOKEN_TILE
    row = lambda i: (i, 0)
    const2 = lambda i: (0, 0)
    wide = lambda dt: jax.ShapeDtypeStruct((n, ATTN_WIDTH), dt)
    out_shape = (wide(BF16), wide(F32), wide(F32), jax.ShapeDtypeStruct((n, IDX_DIM), F32),
                 wide(BF16), wide(BF16), jax.ShapeDtypeStruct((n, IDX_DIM), BF16),
                 wide(BF16), jax.ShapeDtypeStruct((n, LANES), F32), wide(BF16), wide(F32))
    out_specs = tuple(pl.BlockSpec((tm, s.shape[1]), row) for s in out_shape)
    return pl.pallas_call(
        functools.partial(_proj_kernel, sgu_chunk=sgu_chunk),
        grid=(n // tm,),
        in_specs=[
            pl.BlockSpec((tm, D_MODEL), row),
            pl.BlockSpec((1, D_MODEL), const2),
            pl.BlockSpec(w_main.shape, const2),
            pl.BlockSpec(w_small.shape, const2),
            pl.BlockSpec((tm, LANES), lambda i: (i % table_tiles, 0)),
            pl.BlockSpec((tm, LANES), lambda i: (i % table_tiles, 0)),
            pl.BlockSpec((1, SGU_WIDTH), const2),
            pl.BlockSpec((1, SGU_WIDTH), const2),
            pl.BlockSpec(w_sgu.shape, lambda i: (0, 0, 0)),
            pl.BlockSpec(b_sgu.shape, const2),
        ],
        out_specs=out_specs,
        out_shape=out_shape,
        compiler_params=pltpu.CompilerParams(dimension_semantics=("arbitrary",),
                                             vmem_limit_bytes=VMEM_LIMIT_BYTES),
        name="proj_rope_sgu",
    )(x2d, g_attn, w_main, w_small, cos_t, sin_t, ln_g, ln_b, w_sgu, b_sgu)


def _attn_kernel(q_ref, qi_ref, wi_ref, k_ref, v_ref, ki_ref, o_ref, key_ref, bias_ref,
                 *, q_pos_base, s_valid, topk):
    tq = q_ref.shape[1]
    s_pad = k_ref.shape[1]
    ts = KEY_TILE
    rgs = min(MAX_ROW_GROUP, tq)
    j = pl.program_id(1)
    q_pos0 = q_pos_base + j * tq
    k_end = jnp.minimum(((q_pos0 + tq - 1) // CHUNK + 1) * CHUNK, s_valid)
    n_kt = (k_end + ts - 1) // ts

    q_chunk = (q_pos0 + lax.broadcasted_iota(jnp.int32, (tq, ts), 0)) // CHUNK
    k_iota = lax.broadcasted_iota(jnp.int32, (tq, ts), 1)
    wi = wi_ref[0]

    def score_tile(kt, carry):
        k0 = pl.multiple_of(kt * ts, ts)
        ki_t = ki_ref[0, pl.ds(k0, ts), :]
        score = jnp.zeros((tq, ts), F32)
        for h in range(N_IDX_HEADS):
            a = _dot_nt(qi_ref[0, :, h * IDX_DIM:(h + 1) * IDX_DIM], ki_t)
            score = score + jnp.maximum(a, 0.0) * wi[:, IDX_DIM + h:IDX_DIM + h + 1]
        score = score * np.float32(IDX_SCALE)
        score = jnp.where(score == 0.0, 0.0, score)
        k_pos = k0 + k_iota
        allowed = (k_pos // CHUNK <= q_chunk) & (k_pos < s_valid)
        score = jnp.where(allowed, score, -jnp.inf)
        bits = pltpu.bitcast(score, jnp.int32)
        key_ref[:, pl.ds(k0, ts)] = jnp.where(bits < 0, bits ^ jnp.int32(0x7FFFFFFF), bits)
        return carry

    lax.fori_loop(0, n_kt, score_tile, 0)

    def row_group(rg, carry):
        r0 = pl.multiple_of(rg * rgs, rgs)
        rows = pl.ds(r0, rgs)

        def count(pred):
            def tile(kt, acc):
                k0 = pl.multiple_of(kt * ts, ts)
                for c in range(ts // LANES):
                    blk = key_ref[rows, pl.ds(k0 + c * LANES, LANES)]
                    acc = acc + jnp.where(pred(blk, k0 + c * LANES), 1.0, 0.0)
                return acc
            acc = lax.fori_loop(0, n_kt, tile, jnp.zeros((rgs, LANES), F32))
            return jnp.sum(acc, axis=-1, keepdims=True)

        def bisect(i, state):
            lo, cnt_lo = state
            cand = lo + jnp.left_shift(jnp.int32(1), 31 - i)
            cand_b = jnp.broadcast_to(cand, (rgs, LANES))
            cnt = count(lambda blk, _: blk >= cand_b)
            ok = cnt >= topk
            return jnp.where(ok, cand, lo), jnp.where(ok, cnt, cnt_lo)

        lo0 = jnp.full((rgs, 1), jnp.iinfo(jnp.int32).min, jnp.int32)
        cnt0 = jnp.zeros((rgs, 1), F32) + (n_kt * ts).astype(F32)
        lo, cnt_lo = lax.fori_loop(0, 32, bisect, (lo0, cnt0))

        underfull = lo <= NEG_INF_KEY
        thr = jnp.where(underfull, NEG_INF_KEY + 1, lo)
        tied = jnp.logical_and(jnp.logical_not(underfull), cnt_lo > topk)
        any_tied = jnp.max(jnp.where(tied, 1.0, 0.0)) > 0.0

        @pl.when(jnp.logical_not(any_tied))
        def _():
            thr_b = jnp.broadcast_to(thr, (rgs, LANES))

            def tile(kt, c2):
                k0 = pl.multiple_of(kt * ts, ts)
                for c in range(ts // LANES):
                    cols = pl.ds(k0 + c * LANES, LANES)
                    bias_ref[rows, cols] = jnp.where(key_ref[rows, cols] >= thr_b, 0.0, MASK_BIAS)
                return c2
            lax.fori_loop(0, n_kt, tile, 0)

        @pl.when(any_tied)
        def _():
            thr_b = jnp.broadcast_to(thr, (rgs, LANES))
            lane = lax.broadcasted_iota(jnp.int32, (rgs, LANES), 1)
            n_gt = count(lambda blk, _: blk > thr_b)
            need = jnp.where(tied, topk - n_gt, np.float32(s_pad))

            def cut_bisect(i, cut):
                cand = cut + jnp.left_shift(jnp.int32(1), 13 - i)
                cand_b = jnp.broadcast_to(cand, (rgs, LANES))
                n_eq = count(lambda blk, c0: (blk == thr_b) & (lane + c0 < cand_b))
                return jnp.where(n_eq <= need, cand, cut)

            cut = lax.fori_loop(0, 14, cut_bisect, jnp.zeros((rgs, 1), jnp.int32))
            cut_b = jnp.broadcast_to(cut, (rgs, LANES))

            def tile(kt, c2):
                k0 = pl.multiple_of(kt * ts, ts)
                for c in range(ts // LANES):
                    cols = pl.ds(k0 + c * LANES, LANES)
                    blk = key_ref[rows, cols]
                    sel = (blk > thr_b) | ((blk == thr_b) & (lane + (k0 + c * LANES) < cut_b))
                    bias_ref[rows, cols] = jnp.where(sel, 0.0, MASK_BIAS)
                return c2
            lax.fori_loop(0, n_kt, tile, 0)

        return carry

    lax.fori_loop(0, tq // rgs, row_group, 0)

    for h in range(N_HEADS):
        lanes = slice(h * HEAD_DIM, (h + 1) * HEAD_DIM)
        q_h = q_ref[0, :, lanes]

        def kv_tile(kt, state, lanes=lanes, q_h=q_h):
            m, l, acc = state
            k0 = pl.multiple_of(kt * ts, ts)
            s = _dot_nt(q_h, k_ref[0, pl.ds(k0, ts), lanes]) + bias_ref[:, pl.ds(k0, ts)]
            m_new = jnp.maximum(m, jnp.max(s, axis=-1, keepdims=True))
            alpha = jnp.exp(m - m_new)
            p = jnp.exp(s - m_new)
            l = alpha * l + jnp.sum(p, axis=-1, keepdims=True)
            acc = alpha * acc + _dot(p.astype(BF16), v_ref[0, pl.ds(k0, ts), lanes])
            return m_new, l, acc

        init = (jnp.full((tq, 1), -jnp.inf, F32), jnp.zeros((tq, 1), F32), jnp.zeros((tq, HEAD_DIM), F32))
        _, l, acc = lax.fori_loop(0, n_kt, kv_tile, init)
        o_ref[0, :, lanes] = (acc / l).astype(o_ref.dtype)


def _attention(q, qi, wi, k, v, ki, *, tq, q_pos_base, s_valid, topk):
    b, t, _ = q.shape
    s_pad = k.shape[1]
    qblk = lambda bi, j: (bi, j, 0)
    kblk = lambda bi, j: (bi, 0, 0)
    return pl.pallas_call(
        functools.partial(_attn_kernel, q_pos_base=q_pos_base, s_valid=s_valid, topk=topk),
        grid=(b, t // tq),
        in_specs=[
            pl.BlockSpec((1, tq, ATTN_WIDTH), qblk),
            pl.BlockSpec((1, tq, ATTN_WIDTH), qblk),
            pl.BlockSpec((1, tq, LANES), qblk),
            pl.BlockSpec((1, s_pad, ATTN_WIDTH), kblk),
            pl.BlockSpec((1, s_pad, ATTN_WIDTH), kblk),
            pl.BlockSpec((1, s_pad, IDX_DIM), kblk),
        ],
        out_specs=pl.BlockSpec((1, tq, ATTN_WIDTH), qblk),
        out_shape=jax.ShapeDtypeStruct((b, t, ATTN_WIDTH), BF16),
        scratch_shapes=[pltpu.VMEM((tq, s_pad), jnp.int32), pltpu.VMEM((tq, s_pad), F32)],
        compiler_params=pltpu.CompilerParams(dimension_semantics=("arbitrary", "arbitrary"),
                                             vmem_limit_bytes=VMEM_LIMIT_BYTES),
        name="dsa_attention",
    )(q, qi, wi, k, v, ki)


def _ffn_kernel(x_ref, a_ref, s_ref, woa_ref, wos_ref, gf_ref, wg_ref, wu_ref, wd_ref, gl_ref, y_ref):
    h = x_ref[...] + _dot(a_ref[...], woa_ref[...]) + _dot(s_ref[...], wos_ref[...])
    hn = _rms(h, gf_ref[...]).astype(BF16)

    def ff_tile(c, acc):
        gate = _dot(hn, wg_ref[c])
        up = _dot(hn, wu_ref[c])
        act = (gate * jax.nn.sigmoid(gate) * up).astype(BF16)
        return acc + _dot(act, wd_ref[c])

    y = h + lax.fori_loop(0, wg_ref.shape[0], ff_tile, jnp.zeros_like(h))
    y_ref[...] = _rms(y, gl_ref[...])


def _ffn(x2d, attn2d, sgu2d, wo_attn, wo_sgu, g_ffn, wg, wu, wd, g_final):
    n = x2d.shape[0]
    tm = TOKEN_TILE
    row = lambda i: (i, 0)
    const2 = lambda i: (0, 0)
    const3 = lambda i: (0, 0, 0)
    return pl.pallas_call(
        _ffn_kernel,
        grid=(n // tm,),
        in_specs=[
            pl.BlockSpec((tm, D_MODEL), row),
            pl.BlockSpec((tm, ATTN_WIDTH), row),
            pl.BlockSpec((tm, SGU_WIDTH), row),
            pl.BlockSpec(wo_attn.shape, const2, pipeline_mode=pl.Buffered(1)),
            pl.BlockSpec(wo_sgu.shape, const2, pipeline_mode=pl.Buffered(1)),
            pl.BlockSpec((1, D_MODEL), const2),
            pl.BlockSpec(wg.shape, const3, pipeline_mode=pl.Buffered(1)),
            pl.BlockSpec(wu.shape, const3, pipeline_mode=pl.Buffered(1)),
            pl.BlockSpec(wd.shape, const3, pipeline_mode=pl.Buffered(1)),
            pl.BlockSpec((1, D_MODEL), const2),
        ],
        out_specs=pl.BlockSpec((tm, D_MODEL), row),
        out_shape=jax.ShapeDtypeStruct((n, D_MODEL), F32),
        compiler_params=pltpu.CompilerParams(dimension_semantics=("arbitrary",),
                                             vmem_limit_bytes=VMEM_LIMIT_BYTES),
        name="outproj_ffn_norm",
    )(x2d, attn2d, sgu2d, wo_attn, wo_sgu, g_ffn, wg, wu, wd, g_final)


def _rope_tables(pos):
    half = HEAD_DIM // 2
    freqs = ROPE_THETA ** (-jnp.arange(half, dtype=F32) / half)
    ang = pos.astype(F32)[:, None] * freqs[None, :]
    cos = jnp.tile(jnp.cos(ang), (1, LANES // half))
    sin = jnp.tile(jnp.concatenate([-jnp.sin(ang), jnp.sin(ang)], axis=1), (1, LANES // HEAD_DIM))
    return cos, sin


def kernel(x_prompt, x_sample, cache_k, cache_v, cache_idx_k, g_attn, w_in, ln_g, ln_b, w_s, b_s,
           w_out, g_ffn, w_gate, w_up, w_down, g_final):
    bp, tp, _ = x_prompt.shape
    bs, tsmp, _ = x_sample.shape
    past = cache_k.shape[2]
    layer = 0

    w = w_in[layer]
    o_q, o_k, o_v, o_qi, o_ki, o_wi, o_u, o_vs = np.cumsum((0, 512, 512, 512, 512, 64, 8, 512)).tolist()
    w_main = jnp.concatenate([w[:, o_q:o_ki], w[:, o_u:]], axis=1).astype(BF16)
    w_small = jnp.pad(w[:, o_ki:o_u], ((0, 0), (0, LANES - (o_u - o_ki)))).astype(BF16)
    g_a = g_attn[layer][None, :]
    lng = ln_g[layer].reshape(1, SGU_WIDTH)
    lnb = ln_b[layer].reshape(1, SGU_WIDTH)
    wo_attn = w_out[layer][:ATTN_WIDTH].astype(BF16)
    wo_sgu = w_out[layer][ATTN_WIDTH:].astype(BF16)
    n_ff = D_FF // FF_TILE
    wg = w_gate[layer].reshape(D_MODEL, n_ff, FF_TILE).transpose(1, 0, 2).astype(BF16)
    wu = w_up[layer].reshape(D_MODEL, n_ff, FF_TILE).transpose(1, 0, 2).astype(BF16)
    wd = w_down[layer].reshape(n_ff, FF_TILE, D_MODEL).astype(BF16)
    g_f = g_ffn[layer][None, :]
    g_l = g_final[None, :]

    def run(x, pos, sgu_chunk, attend):
        b, t, _ = x.shape
        x2d = x.reshape(b * t, D_MODEL)
        cos_t, sin_t = _rope_tables(pos)
        reps = max(1, TOKEN_TILE // t)
        cos_t, sin_t = jnp.tile(cos_t, (reps, 1)), jnp.tile(sin_t, (reps, 1))
        w_sgu = w_s[layer][:, :sgu_chunk, :sgu_chunk]
        b_sgu = b_s[layer][:, :sgu_chunk].T
        q, k, v, ki, kb, vb, kib, qi, wi, sgu, vn = _project(
            x2d, g_a, w_main, w_small, cos_t, sin_t, lng, lnb, w_sgu, b_sgu,
            sgu_chunk=sgu_chunk, table_tiles=cos_t.shape[0] // TOKEN_TILE)
        r3 = lambda a: a.reshape(b, t, a.shape[-1])
        attn = attend(r3(q), r3(qi), r3(wi), r3(kb), r3(vb), r3(kib))
        y = _ffn(x2d, attn.reshape(b * t, ATTN_WIDTH), sgu, wo_attn, wo_sgu, g_f, wg, wu, wd, g_l)
        new_k = k.reshape(1, b, t, N_HEADS, HEAD_DIM)
        new_v = v.reshape(1, b, t, N_HEADS, HEAD_DIM)
        new_ki = ki.reshape(1, b, t, IDX_DIM)
        new_vn = vn.reshape(1, b, t, SGU_GROUPS, SGU_GROUP_DIM)
        return y.reshape(b, t, D_MODEL), new_k, new_v, new_ki, new_vn

    def attend_prompt(q, qi, wi, kb, vb, kib):
        return _attention(q, qi, wi, kb, vb, kib, tq=256, q_pos_base=0, s_valid=tp,
                          topk=min(MAX_TOPK, tp // 4))

    def attend_sample(q, qi, wi, kb, vb, kib):
        s_valid = past + tsmp
        s_pad = -(-s_valid // KEY_TILE) * KEY_TILE

        def keys(cache, new):
            full = jnp.concatenate([cache.reshape(bs, past, -1).astype(BF16), new], axis=1)
            return jnp.pad(full, ((0, 0), (0, s_pad - s_valid), (0, 0)))

        return _attention(q, qi, wi, keys(cache_k[layer], kb), keys(cache_v[layer], vb),
                          keys(cache_idx_k[layer], kib), tq=tsmp, q_pos_base=past, s_valid=s_valid,
                          topk=min(MAX_TOPK, s_valid // 4))

    y_p, k_p, v_p, ki_p, _ = run(x_prompt, jnp.arange(tp, dtype=jnp.int32), min(tp, SGU_CHUNK), attend_prompt)
    y_s, k_s, v_s, ki_s, vn_s = run(x_sample, past + jnp.arange(tsmp, dtype=jnp.int32),
                                    min(tsmp, SGU_CHUNK), attend_sample)
    return (y_p, y_s, k_p, v_p, ki_p, k_s, v_s, ki_s, vn_s)
```

```python
import functools

import jax
import jax.numpy as jnp
import numpy as np
from jax import lax
from jax.experimental import pallas as pl
from jax.experimental.pallas import tpu as pltpu

D_MODEL = 1024
CHUNK = 64
HEAD_DIM = 64
N_HEADS = 8
ATTN_WIDTH = N_HEADS * HEAD_DIM
N_IDX_HEADS = 8
IDX_DIM = 64
MAX_TOPK = 256
SGU_GROUPS = 4
SGU_GROUP_DIM = 128
SGU_WIDTH = SGU_GROUPS * SGU_GROUP_DIM
SGU_CHUNK = 128
D_FF = 2816
ROPE_THETA = 10000.0
RMS_EPS = 1e-6
LN_EPS = 1e-5

LANES = 128
VMEM_LIMIT_BYTES = 56 * 1024 * 1024

TOKEN_TILE = 512
KEY_TILE = 512
COLS_PER_TILE = KEY_TILE // LANES
SCORE_ROWS = 32
SCORE_TILE = 256
MAX_ROW_GROUP = 64
SOFTMAX_ROWS = 128
STAT_ROWS = 128
FF_TILE = 256

INT32_MIN = -2 ** 31
NEG_INF_KEY = -0x7F800000
MASK_BIAS = -1e30
LOG2E = 1.4426950408889634
LOGIT_LIMIT = 40.0
VALUE_LIMIT = 1e18
IDX_SCALE = (IDX_DIM ** -0.5) * (N_IDX_HEADS ** -0.5)
BF16 = jnp.bfloat16
F32 = jnp.float32


def _dot(a, b):
    return jnp.dot(a, b, preferred_element_type=F32)


def _dot_nt(a, b):
    return lax.dot_general(a, b, (((1,), (1,)), ((), ())), preferred_element_type=F32)


def _rms(x, g):
    return x * lax.rsqrt(jnp.mean(x * x, axis=-1, keepdims=True) + RMS_EPS) * g


def _gelu_tanh(x):
    c = np.float32(np.sqrt(2.0 / np.pi))
    return 0.5 * x * (1.0 + jnp.tanh(c * (x + np.float32(0.044715) * (x * x * x))))


def _rope_lanes(x, cos, sin_signed, first_half):
    partner = jnp.where(first_half, pltpu.roll(x, 96, 1), pltpu.roll(x, 32, 1))
    return x * cos + partner * sin_signed


def _proj_kernel(x_ref, g_ref, w_ref, ws_ref, cos_ref, sin_ref, lng_ref, lnb_ref, wsgu_ref, bsgu_ref,
                 q_ref, k_ref, v_ref, ki_ref, kb_ref, vb_ref, kib_ref, qi_ref, wi_ref, sgu_ref, vn_ref,
                 *, sgu_chunk):
    tm = x_ref.shape[0]
    xn = _rms(x_ref[...], g_ref[...]).astype(BF16)
    cos = cos_ref[...]
    sin = sin_ref[...]
    lane = lax.broadcasted_iota(jnp.int32, (tm, LANES), 1)
    first_half = (lane % HEAD_DIM) < (HEAD_DIM // 2)

    def rope(r):
        return jnp.concatenate(
            [_rope_lanes(r[:, c * LANES:(c + 1) * LANES], cos, sin, first_half)
             for c in range(r.shape[1] // LANES)], axis=1)

    def section(i):
        return _dot(xn, w_ref[:, i * ATTN_WIDTH:(i + 1) * ATTN_WIDTH])

    q = rope(section(0))
    q_ref[...] = (q * np.float32(LOG2E * HEAD_DIM ** -0.5)).astype(BF16)
    k = rope(section(1))
    k_ref[...] = k
    kb_ref[...] = k.astype(BF16)
    v = section(2)
    v_ref[...] = v
    vb_ref[...] = v.astype(BF16)
    qi_ref[...] = rope(section(3)).astype(BF16)

    small = _dot(xn, ws_ref[...])
    ki = _rope_lanes(small, cos, sin, first_half)[:, :IDX_DIM]
    ki_ref[...] = ki
    kib_ref[...] = ki.astype(BF16)
    wi_ref[...] = small

    u = _gelu_tanh(section(4))
    vs = _gelu_tanh(section(5))
    ii = lax.broadcasted_iota(jnp.int32, (sgu_chunk, sgu_chunk), 0)
    jj = lax.broadcasted_iota(jnp.int32, (sgu_chunk, sgu_chunk), 1)
    causal = (jj // CHUNK) <= (ii // CHUNK)
    for g in range(SGU_GROUPS):
        lanes = slice(g * SGU_GROUP_DIM, (g + 1) * SGU_GROUP_DIM)
        vg = vs[:, lanes]
        mu = jnp.mean(vg, axis=-1, keepdims=True)
        var = jnp.mean(jnp.square(vg - mu), axis=-1, keepdims=True)
        vn = (vg - mu) * lax.rsqrt(var + LN_EPS) * lng_ref[:, lanes] + lnb_ref[:, lanes]
        vn_ref[:, lanes] = vn
        vnb = vn.astype(BF16)
        wm = jnp.where(causal, wsgu_ref[g], 0.0).astype(BF16)
        bias = bsgu_ref[:, g:g + 1]
        for c in range(tm // sgu_chunk):
            rows = slice(c * sgu_chunk, (c + 1) * sgu_chunk)
            s = _dot(wm, vnb[rows]) + bias
            sgu_ref[rows, lanes] = (u[rows, lanes] * s).astype(BF16)


def _project(x2d, g_attn, w_main, w_small, cos_t, sin_t, ln_g, ln_b, w_sgu, b_sgu, *, sgu_chunk, table_tiles):
    n = x2d.shape[0]
    tm = TOKEN_TILE
    row = lambda i: (i, 0)
    const2 = lambda i: (0, 0)
    wide = lambda dt: jax.ShapeDtypeStruct((n, ATTN_WIDTH), dt)
    out_shape = (wide(BF16), wide(F32), wide(F32), jax.ShapeDtypeStruct((n, IDX_DIM), F32),
                 wide(BF16), wide(BF16), jax.ShapeDtypeStruct((n, IDX_DIM), BF16),
                 wide(BF16), jax.ShapeDtypeStruct((n, LANES), F32), wide(BF16), wide(F32))
    out_specs = tuple(pl.BlockSpec((tm, s.shape[1]), row) for s in out_shape)
    return pl.pallas_call(
        functools.partial(_proj_kernel, sgu_chunk=sgu_chunk),
        grid=(n // tm,),
        in_specs=[
            pl.BlockSpec((tm, D_MODEL), row),
            pl.BlockSpec((1, D_MODEL), const2),
            pl.BlockSpec(w_main.shape, const2),
            pl.BlockSpec(w_small.shape, const2),
            pl.BlockSpec((tm, LANES), lambda i: (i % table_tiles, 0)),
            pl.BlockSpec((tm, LANES), lambda i: (i % table_tiles, 0)),
            pl.BlockSpec((1, SGU_WIDTH), const2),
            pl.BlockSpec((1, SGU_WIDTH), const2),
            pl.BlockSpec(w_sgu.shape, lambda i: (0, 0, 0)),
            pl.BlockSpec(b_sgu.shape, const2),
        ],
        out_specs=out_specs,
        out_shape=out_shape,
        compiler_params=pltpu.CompilerParams(dimension_semantics=("arbitrary",),
                                             vmem_limit_bytes=VMEM_LIMIT_BYTES),
        name="proj_rope_sgu",
    )(x2d, g_attn, w_main, w_small, cos_t, sin_t, ln_g, ln_b, w_sgu, b_sgu)


def _head_indicator():
    d_head = lax.broadcasted_iota(jnp.int32, (ATTN_WIDTH, LANES), 0) // HEAD_DIM
    h = lax.broadcasted_iota(jnp.int32, (ATTN_WIDTH, LANES), 1)
    return jnp.where(d_head == h, 1.0, 0.0).astype(BF16)


def _attn_kernel(q_ref, qi_ref, wi_ref, k_ref, v_ref, ki_ref, o_ref,
                 key_ref, bias_ref, stat_ref, lhs_ref, wb_ref, acc_ref, *, q_pos_base, s_valid, topk):
    tq = q_ref.shape[1]
    s_pad = k_ref.shape[1]
    ts = KEY_TILE
    j = pl.program_id(1)
    q_pos0 = q_pos_base + j * tq
    k_end = jnp.minimum(((q_pos0 + tq - 1) // CHUNK + 1) * CHUNK, s_valid)
    n_kt = (k_end + ts - 1) // ts

    def bias_tile(kt, rows):
        return jnp.concatenate([bias_ref[kt * COLS_PER_TILE + c, rows, :] for c in range(COLS_PER_TILE)], axis=1)

    @pl.when(j == 0)
    def _():
        head_sum = _head_indicator()

        def stat_tile(t, state):
            kmax, vmax = state
            rows = pl.ds(pl.multiple_of(t * STAT_ROWS, STAT_ROWS), STAT_ROWS)
            kf = k_ref[0, rows, :].astype(F32)
            vf = jnp.abs(v_ref[0, rows, :].astype(F32))
            vm = vf[:, :LANES]
            for c in range(1, ATTN_WIDTH // LANES):
                vm = jnp.maximum(vm, vf[:, c * LANES:(c + 1) * LANES])
            return jnp.maximum(kmax, _dot((kf * kf).astype(BF16), head_sum)), jnp.maximum(vmax, vm)

        zero = jnp.zeros((STAT_ROWS, LANES), F32)
        kmax, vmax = lax.fori_loop(0, s_pad // STAT_ROWS, stat_tile, (zero, zero))
        stat_ref[0:1, :] = jnp.max(kmax, axis=0, keepdims=True)
        stat_ref[1:2, :] = jnp.max(vmax, axis=0, keepdims=True)

    ra = min(SCORE_ROWS, tq)
    n_rb = tq // ra
    wi = wi_ref[0]
    for h in range(N_IDX_HEADS):
        wb_ref[:, h * LANES:(h + 1) * LANES] = jnp.broadcast_to(wi[:, IDX_DIM + h:IDX_DIM + h + 1], (tq, LANES))
    k_lim = []
    for rb in range(n_rb):
        rows = slice(rb * ra, (rb + 1) * ra)
        lhs_ref[rb] = jnp.concatenate(
            [qi_ref[0, rows, h * IDX_DIM:(h + 1) * IDX_DIM] for h in range(N_IDX_HEADS)], axis=0)
        q_pos = q_pos0 + rb * ra + lax.broadcasted_iota(jnp.int32, (ra, LANES), 0)
        k_lim.append(jnp.minimum((q_pos // CHUNK + 1) * CHUNK, s_valid))
    k_iota = lax.broadcasted_iota(jnp.int32, (ra, LANES), 1)

    def score_tile(kt, carry):
        for half in range(ts // SCORE_TILE):
            t = kt * (ts // SCORE_TILE) + half
            k0 = pl.multiple_of(t * SCORE_TILE, SCORE_TILE)
            ki_t = ki_ref[0, pl.ds(k0, SCORE_TILE), :]
            for rb in range(n_rb):
                rows = slice(rb * ra, (rb + 1) * ra)
                a = _dot_nt(lhs_ref[rb], ki_t)
                for c in range(SCORE_TILE // LANES):
                    lanes = slice(c * LANES, (c + 1) * LANES)
                    score = jnp.maximum(a[0:ra, lanes], 0.0) * wb_ref[rows, 0:LANES]
                    for h in range(1, N_IDX_HEADS):
                        score = score + (jnp.maximum(a[h * ra:(h + 1) * ra, lanes], 0.0)
                                         * wb_ref[rows, h * LANES:(h + 1) * LANES])
                    bits = pltpu.bitcast(score * np.float32(IDX_SCALE), jnp.int32)
                    key = jnp.where(bits < 0, jnp.int32(INT32_MIN) - bits, bits)
                    key = jnp.where(k_iota < k_lim[rb] - (k0 + c * LANES), key, NEG_INF_KEY)
                    key_ref[t * (SCORE_TILE // LANES) + c, rows, :] = key
        return carry

    lax.fori_loop(0, n_kt, score_tile, 0)

    rgs = min(MAX_ROW_GROUP, tq)

    def row_group(rg, carry):
        rows = pl.ds(pl.multiple_of(rg * rgs, rgs), rgs)

        def count(pred):
            def tile(kt, accs):
                accs = list(accs)
                for c in range(COLS_PER_TILE):
                    col = kt * COLS_PER_TILE + c
                    hit = jnp.where(pred(key_ref[col, rows, :], col * LANES), 1.0, 0.0)
                    accs[c % 2] = accs[c % 2] + hit
                return tuple(accs)
            zero = jnp.zeros((rgs, LANES), F32)
            a0, a1 = lax.fori_loop(0, n_kt, tile, (zero, zero))
            return jnp.sum(a0 + a1, axis=-1, keepdims=True)

        def bisect(i, state):
            lo, cnt_lo = state
            cand = lo + jnp.left_shift(jnp.int32(1), 31 - i)
            cand_b = jnp.broadcast_to(cand, (rgs, LANES))
            cnt = count(lambda blk, _: blk >= cand_b)
            ok = cnt >= topk
            return jnp.where(ok, cand, lo), jnp.where(ok, cnt, cnt_lo)

        lo0 = jnp.full((rgs, 1), INT32_MIN, jnp.int32)
        cnt0 = jnp.zeros((rgs, 1), F32) + (n_kt * ts).astype(F32)
        lo, cnt_lo = lax.fori_loop(0, 32, bisect, (lo0, cnt0))

        underfull = lo <= NEG_INF_KEY
        thr = jnp.where(underfull, NEG_INF_KEY + 1, lo)
        tied = jnp.logical_and(jnp.logical_not(underfull), cnt_lo > topk)
        any_tied = jnp.max(jnp.where(tied, 1.0, 0.0)) > 0.0
        thr_b = jnp.broadcast_to(thr, (rgs, LANES))

        @pl.when(jnp.logical_not(any_tied))
        def _():
            def tile(kt, c2):
                for c in range(COLS_PER_TILE):
                    col = kt * COLS_PER_TILE + c
                    bias_ref[col, rows, :] = jnp.where(key_ref[col, rows, :] >= thr_b, 0.0, MASK_BIAS)
                return c2
            lax.fori_loop(0, n_kt, tile, 0)

        @pl.when(any_tied)
        def _():
            lane = lax.broadcasted_iota(jnp.int32, (rgs, LANES), 1)
            n_gt = count(lambda blk, _: blk > thr_b)
            need = jnp.where(tied, topk - n_gt, np.float32(s_pad))

            def cut_bisect(i, cut):
                cand = cut + jnp.left_shift(jnp.int32(1), 13 - i)
                cand_b = jnp.broadcast_to(cand, (rgs, LANES))
                n_eq = count(lambda blk, c0: (blk == thr_b) & (lane + c0 < cand_b))
                return jnp.where(n_eq <= need, cand, cut)

            cut = lax.fori_loop(0, 14, cut_bisect, jnp.zeros((rgs, 1), jnp.int32))
            cut_b = jnp.broadcast_to(cut, (rgs, LANES))

            def tile(kt, c2):
                for c in range(COLS_PER_TILE):
                    col = kt * COLS_PER_TILE + c
                    blk = key_ref[col, rows, :]
                    sel = (blk > thr_b) | ((blk == thr_b) & (lane + col * LANES < cut_b))
                    bias_ref[col, rows, :] = jnp.where(sel, 0.0, MASK_BIAS)
                return c2
            lax.fori_loop(0, n_kt, tile, 0)

        return carry

    lax.fori_loop(0, tq // rgs, row_group, 0)

    qf = q_ref[0].astype(F32)
    qn2 = jnp.max(_dot((qf * qf).astype(BF16), _head_indicator()), axis=0, keepdims=True)
    logit2_bound = jnp.max(qn2 * stat_ref[0:1, :]) * np.float32(1.05)
    in_range = jnp.logical_and(logit2_bound <= LOGIT_LIMIT * LOGIT_LIMIT,
                               jnp.max(stat_ref[1:2, :]) <= VALUE_LIMIT)

    @pl.when(in_range)
    def _():
        all_rows = slice(None)
        lane_q = lax.broadcasted_iota(jnp.int32, (tq, LANES), 1)
        lane_k = lax.broadcasted_iota(jnp.int32, (ts, LANES), 1)
        one = jnp.ones((ts, LANES), BF16)
        pair_lanes = [slice(pair * LANES, (pair + 1) * LANES) for pair in range(N_HEADS // 2)]
        acc_ref[...] = jnp.zeros_like(acc_ref)

        def kv_tile(kt, carry):
            keys = pl.ds(pl.multiple_of(kt * ts, ts), ts)
            bias = bias_tile(kt, all_rows)
            for pair, lanes in enumerate(pair_lanes):
                qp = q_ref[0, :, lanes]
                kp = k_ref[0, keys, lanes]
                vp = v_ref[0, keys, lanes]
                q_e = jnp.where(lane_q < HEAD_DIM, qp, jnp.zeros_like(qp))
                p_e = jnp.exp2(_dot_nt(q_e, kp) + bias).astype(BF16)
                acc_ref[2 * pair] += _dot(p_e, jnp.where(lane_k < HEAD_DIM, vp, one))
                q_o = jnp.where(lane_q >= HEAD_DIM, qp, jnp.zeros_like(qp))
                p_o = jnp.exp2(_dot_nt(q_o, kp) + bias).astype(BF16)
                acc_ref[2 * pair + 1] += _dot(p_o, jnp.where(lane_k >= HEAD_DIM, vp, one))
            return carry

        lax.fori_loop(0, n_kt, kv_tile, 0)
        for pair, lanes in enumerate(pair_lanes):
            acc_e = acc_ref[2 * pair]
            acc_o = acc_ref[2 * pair + 1]
            out = jnp.where(lane_q < HEAD_DIM, acc_e / pltpu.roll(acc_e, HEAD_DIM, 1),
                            acc_o / pltpu.roll(acc_o, HEAD_DIM, 1))
            o_ref[0, :, lanes] = out.astype(o_ref.dtype)

    @pl.when(jnp.logical_not(in_range))
    def _():
        all_rows = slice(None)
        for h in range(N_HEADS):
            lanes = slice(h * HEAD_DIM, (h + 1) * HEAD_DIM)
            q_h = q_ref[0, :, lanes]

            def kv_tile(kt, state, lanes=lanes, q_h=q_h):
                m, l, acc = state
                keys = pl.ds(pl.multiple_of(kt * ts, ts), ts)
                s = _dot_nt(q_h, k_ref[0, keys, lanes]) + bias_tile(kt, all_rows)
                m_new = jnp.maximum(m, jnp.max(s, axis=-1, keepdims=True))
                alpha = jnp.exp2(m - m_new)
                p = jnp.exp2(s - m_new)
                l = alpha * l + jnp.sum(p, axis=-1, keepdims=True)
                acc = alpha * acc + _dot(p.astype(BF16), v_ref[0, keys, lanes])
                return m_new, l, acc

            init = (jnp.full((tq, 1), -jnp.inf, F32), jnp.zeros((tq, 1), F32), jnp.zeros((tq, HEAD_DIM), F32))
            _, l, acc = lax.fori_loop(0, n_kt, kv_tile, init)
            o_ref[0, :, lanes] = (acc / l).astype(o_ref.dtype)


def _attention(q, qi, wi, k, v, ki, *, tq, q_pos_base, s_valid, topk):
    b, t, _ = q.shape
    s_pad = k.shape[1]
    qblk = lambda bi, j: (bi, j, 0)
    kblk = lambda bi, j: (bi, 0, 0)
    return pl.pallas_call(
        functools.partial(_attn_kernel, q_pos_base=q_pos_base, s_valid=s_valid, topk=topk),
        grid=(b, t // tq),
        in_specs=[
            pl.BlockSpec((1, tq, ATTN_WIDTH), qblk),
            pl.BlockSpec((1, tq, ATTN_WIDTH), qblk),
            pl.BlockSpec((1, tq, LANES), qblk),
            pl.BlockSpec((1, s_pad, ATTN_WIDTH), kblk),
            pl.BlockSpec((1, s_pad, ATTN_WIDTH), kblk),
            pl.BlockSpec((1, s_pad, IDX_DIM), kblk),
        ],
        out_specs=pl.BlockSpec((1, tq, ATTN_WIDTH), qblk),
        out_shape=jax.ShapeDtypeStruct((b, t, ATTN_WIDTH), BF16),
        scratch_shapes=[pltpu.VMEM((s_pad // LANES, tq, LANES), jnp.int32),
                        pltpu.VMEM((s_pad // LANES, tq, LANES), F32),
                        pltpu.VMEM((8, LANES), F32),
                        pltpu.VMEM((tq // min(SCORE_ROWS, tq), N_IDX_HEADS * min(SCORE_ROWS, tq), IDX_DIM), BF16),
                        pltpu.VMEM((tq, N_IDX_HEADS * LANES), F32),
                        pltpu.VMEM((N_HEADS, tq, LANES), F32)],
        compiler_params=pltpu.CompilerParams(dimension_semantics=("arbitrary", "arbitrary"),
                                             vmem_limit_bytes=VMEM_LIMIT_BYTES),
        name="dsa_attention",
    )(q, qi, wi, k, v, ki)


def _ffn_kernel(x_ref, a_ref, s_ref, woa_ref, wos_ref, gf_ref, wg_ref, wu_ref, wd_ref, gl_ref, y_ref):
    h = x_ref[...] + _dot(a_ref[...], woa_ref[...]) + _dot(s_ref[...], wos_ref[...])
    hn = _rms(h, gf_ref[...]).astype(BF16)

    def ff_tile(c, acc):
        gate = _dot(hn, wg_ref[c])
        up = _dot(hn, wu_ref[c])
        act = (gate * jax.nn.sigmoid(gate) * up).astype(BF16)
        return acc + _dot(act, wd_ref[c])

    y = h + lax.fori_loop(0, wg_ref.shape[0], ff_tile, jnp.zeros_like(h))
    y_ref[...] = _rms(y, gl_ref[...])


def _ffn(x2d, attn2d, sgu2d, wo_attn, wo_sgu, g_ffn, wg, wu, wd, g_final):
    n = x2d.shape[0]
    tm = TOKEN_TILE
    row = lambda i: (i, 0)
    const2 = lambda i: (0, 0)
    const3 = lambda i: (0, 0, 0)
    return pl.pallas_call(
        _ffn_kernel,
        grid=(n // tm,),
        in_specs=[
            pl.BlockSpec((tm, D_MODEL), row),
            pl.BlockSpec((tm, ATTN_WIDTH), row),
            pl.BlockSpec((tm, SGU_WIDTH), row),
            pl.BlockSpec(wo_attn.shape, const2, pipeline_mode=pl.Buffered(1)),
            pl.BlockSpec(wo_sgu.shape, const2, pipeline_mode=pl.Buffered(1)),
            pl.BlockSpec((1, D_MODEL), const2),
            pl.BlockSpec(wg.shape, const3, pipeline_mode=pl.Buffered(1)),
            pl.BlockSpec(wu.shape, const3, pipeline_mode=pl.Buffered(1)),
            pl.BlockSpec(wd.shape, const3, pipeline_mode=pl.Buffered(1)),
            pl.BlockSpec((1, D_MODEL), const2),
        ],
        out_specs=pl.BlockSpec((tm, D_MODEL), row),
        out_shape=jax.ShapeDtypeStruct((n, D_MODEL), F32),
        compiler_params=pltpu.CompilerParams(dimension_semantics=("arbitrary",),
                                             vmem_limit_bytes=VMEM_LIMIT_BYTES),
        name="outproj_ffn_norm",
    )(x2d, attn2d, sgu2d, wo_attn, wo_sgu, g_ffn, wg, wu, wd, g_final)


def _rope_tables(pos):
    half = HEAD_DIM // 2
    freqs = ROPE_THETA ** (-jnp.arange(half, dtype=F32) / half)
    ang = pos.astype(F32)[:, None] * freqs[None, :]
    cos = jnp.tile(jnp.cos(ang), (1, LANES // half))
    sin = jnp.tile(jnp.concatenate([-jnp.sin(ang), jnp.sin(ang)], axis=1), (1, LANES // HEAD_DIM))
    return cos, sin


def kernel(x_prompt, x_sample, cache_k, cache_v, cache_idx_k, g_attn, w_in, ln_g, ln_b, w_s, b_s,
           w_out, g_ffn, w_gate, w_up, w_down, g_final):
    bp, tp, _ = x_prompt.shape
    bs, tsmp, _ = x_sample.shape
    past = cache_k.shape[2]
    layer = 0

    w = w_in[layer]
    o_q, o_k, o_v, o_qi, o_ki, o_wi, o_u, o_vs = np.cumsum((0, 512, 512, 512, 512, 64, 8, 512)).tolist()
    w_main = jnp.concatenate([w[:, o_q:o_ki], w[:, o_u:]], axis=1).astype(BF16)
    w_small = jnp.pad(w[:, o_ki:o_u], ((0, 0), (0, LANES - (o_u - o_ki)))).astype(BF16)
    g_a = g_attn[layer][None, :]
    lng = ln_g[layer].reshape(1, SGU_WIDTH)
    lnb = ln_b[layer].reshape(1, SGU_WIDTH)
    wo_attn = w_out[layer][:ATTN_WIDTH].astype(BF16)
    wo_sgu = w_out[layer][ATTN_WIDTH:].astype(BF16)
    n_ff = D_FF // FF_TILE
    wg = w_gate[layer].reshape(D_MODEL, n_ff, FF_TILE).transpose(1, 0, 2).astype(BF16)
    wu = w_up[layer].reshape(D_MODEL, n_ff, FF_TILE).transpose(1, 0, 2).astype(BF16)
    wd = w_down[layer].reshape(n_ff, FF_TILE, D_MODEL).astype(BF16)
    g_f = g_ffn[layer][None, :]
    g_l = g_final[None, :]

    def run(x, pos, sgu_chunk, attend):
        b, t, _ = x.shape
        x2d = x.reshape(b * t, D_MODEL)
        cos_t, sin_t = _rope_tables(pos)
        reps = max(1, TOKEN_TILE // t)
        cos_t, sin_t = jnp.tile(cos_t, (reps, 1)), jnp.tile(sin_t, (reps, 1))
        w_sgu = w_s[layer][:, :sgu_chunk, :sgu_chunk]
        b_sgu = b_s[layer][:, :sgu_chunk].T
        q, k, v, ki, kb, vb, kib, qi, wi, sgu, vn = _project(
            x2d, g_a, w_main, w_small, cos_t, sin_t, lng, lnb, w_sgu, b_sgu,
            sgu_chunk=sgu_chunk, table_tiles=cos_t.shape[0] // TOKEN_TILE)
        r3 = lambda a: a.reshape(b, t, a.shape[-1])
        attn = attend(r3(q), r3(qi), r3(wi), r3(kb), r3(vb), r3(kib))
        y = _ffn(x2d, attn.reshape(b * t, ATTN_WIDTH), sgu, wo_attn, wo_sgu, g_f, wg, wu, wd, g_l)
        new_k = k.reshape(1, b, t, N_HEADS, HEAD_DIM)
        new_v = v.reshape(1, b, t, N_HEADS, HEAD_DIM)
        new_ki = ki.reshape(1, b, t, IDX_DIM)
        new_vn = vn.reshape(1, b, t, SGU_GROUPS, SGU_GROUP_DIM)
        return y.reshape(b, t, D_MODEL), new_k, new_v, new_ki, new_vn

    def attend_prompt(q, qi, wi, kb, vb, kib):
        return _attention(q, qi, wi, kb, vb, kib, tq=256, q_pos_base=0, s_valid=tp,
                          topk=min(MAX_TOPK, tp // 4))

    def attend_sample(q, qi, wi, kb, vb, kib):
        s_valid = past + tsmp
        s_pad = -(-s_valid // KEY_TILE) * KEY_TILE

        def keys(cache, new):
            full = jnp.concatenate([cache.reshape(bs, past, -1).astype(BF16), new], axis=1)
            return jnp.pad(full, ((0, 0), (0, s_pad - s_valid), (0, 0)))

        return _attention(q, qi, wi, keys(cache_k[layer], kb), keys(cache_v[layer], vb),
                          keys(cache_idx_k[layer], kib), tq=tsmp, q_pos_base=past, s_valid=s_valid,
                          topk=min(MAX_TOPK, s_valid // 4))

    y_p, k_p, v_p, ki_p, _ = run(x_prompt, jnp.arange(tp, dtype=jnp.int32), min(tp, SGU_CHUNK), attend_prompt)
    y_s, k_s, v_s, ki_s, vn_s = run(x_sample, past + jnp.arange(tsmp, dtype=jnp.int32),
                                    min(tsmp, SGU_CHUNK), attend_sample)
    return (y_p, y_s, k_p, v_p, ki_p, k_s, v_s, ki_s, vn_s)
```
